```python
import jax, jax.numpy as jnp
from jax import lax
import numpy as np

D_MODEL = 1024
BATCH = 8
SEQ = 4096
DEPTH = 1

HEAD_DIM = 64
MIX_WIDTH = D_MODEL
ATT_WIDTH = MIX_WIDTH // 2
CONV_WIDTH = MIX_WIDTH - ATT_WIDTH
N_ATT_HEADS = ATT_WIDTH // HEAD_DIM
N_CONV_GROUPS = CONV_WIDTH // HEAD_DIM
N_GROUPS = N_ATT_HEADS + N_CONV_GROUPS
CONV_K = 3
D_FF = 256 * ((8 * D_MODEL // 3 + 255) // 256)
Q_BLOCK = 128
N_MOD = 9
EPS = 1e-6
IN_COLS = 3 * ATT_WIDTH + N_ATT_HEADS + 3 * CONV_WIDTH

kernel_name = "hybrid_fox_shortconv_macaron_adaln"


def rmsnorm(x, g):
    xf = x.astype(jnp.float32)
    y = xf * lax.rsqrt(jnp.mean(xf * xf, axis=-1, keepdims=True) + EPS)
    return (y * g.astype(jnp.float32)).astype(x.dtype)


def modulate(h, shift, scale):
    return h * (1.0 + scale[:, None, :]) + shift[:, None, :]


def swiglu(h, w_gate, w_up, w_down):
    return (jax.nn.silu(h @ w_gate) * (h @ w_up)) @ w_down


def forgetting_attention(q, k, v, log_f):
    S = q.shape[2]
    scale = 1.0 / np.sqrt(HEAD_DIM).astype(np.float32)
    F = jnp.cumsum(log_f, axis=-1)
    outs = []
    for i in range(S // Q_BLOCK):
        q0, q1 = i * Q_BLOCK, (i + 1) * Q_BLOCK
        qb = q[:, :, q0:q1]
        kb = k[:, :, :q1]
        vb = v[:, :, :q1]
        s = jnp.einsum('bhqd,bhkd->bhqk', qb, kb).astype(jnp.float32) * scale
        s = s + F[:, :, q0:q1, None] - F[:, :, None, :q1]
        qpos = q0 + jnp.arange(Q_BLOCK)
        kpos = jnp.arange(q1)
        s = jnp.where(kpos[None, :] <= qpos[:, None], s, -jnp.inf)
        p = jax.nn.softmax(s, axis=-1)
        outs.append(jnp.einsum('bhqk,bhkd->bhqd', p.astype(v.dtype), vb))
    return jnp.concatenate(outs, axis=2)


def short_conv(u, conv_w):
    S = u.shape[1]
    up = jnp.pad(u, ((0, 0), (CONV_K - 1, 0), (0, 0)))
    y = conv_w[0] * up[:, 0:S]
    for j in range(1, CONV_K):
        y = y + conv_w[j] * up[:, j:j + S]
    return y


def hybrid_mixer(h, w_in, forget_bias, conv_w, group_norm_g, w_out):
    B, S, _ = h.shape
    proj = h @ w_in
    o = 0
    q = proj[..., o:o + ATT_WIDTH]; o += ATT_WIDTH
    k = proj[..., o:o + ATT_WIDTH]; o += ATT_WIDTH
    v = proj[..., o:o + ATT_WIDTH]; o += ATT_WIDTH
    f_logit = proj[..., o:o + N_ATT_HEADS]; o += N_ATT_HEADS
    gate_b = proj[..., o:o + CONV_WIDTH]; o += CONV_WIDTH
    gate_c = proj[..., o:o + CONV_WIDTH]; o += CONV_WIDTH
    xc = proj[..., o:o + CONV_WIDTH]

    def heads(t):
        return t.reshape(B, S, N_ATT_HEADS, HEAD_DIM).transpose(0, 2, 1, 3)
    log_f = jax.nn.log_sigmoid(f_logit.astype(jnp.float32) + forget_bias.astype(jnp.float32))
    log_f = log_f.transpose(0, 2, 1)
    att = forgetting_attention(heads(q), heads(k), heads(v), log_f)
    att = att.transpose(0, 2, 1, 3).reshape(B, S, ATT_WIDTH)

    cv = gate_b * short_conv(gate_c * xc, conv_w)

    y = jnp.concatenate([att, cv], axis=-1).reshape(B, S, N_GROUPS, HEAD_DIM)
    y = rmsnorm(y, group_norm_g.reshape(N_GROUPS, HEAD_DIM))
    return y.reshape(B, S, MIX_WIDTH) @ w_out


def setup_inputs(seed: int = 0) -> dict:
    key = jax.random.key(seed)
    ks = jax.random.split(key, 20)
    f32 = jnp.float32
    D = D_MODEL

    def nrm(k, shape, fan_in):
        return jax.random.normal(k, shape, f32) * (fan_in ** -0.5)

    def gain(k, n):
        return 1.0 + 0.02 * jax.random.normal(k, (n,), f32)

    return {
        "x": jax.random.normal(ks[0], (BATCH, SEQ, D), f32),
        "c": jax.random.normal(ks[1], (BATCH, D), f32),
        "ada_w": nrm(ks[2], (D, N_MOD * D), D),
        "ada_b": 0.02 * jax.random.normal(ks[3], (N_MOD * D,), f32),
        "norm1_g": gain(ks[4], D),
        "ffn1_w_gate": nrm(ks[5], (D, D_FF), D),
        "ffn1_w_up": nrm(ks[6], (D, D_FF), D),
        "ffn1_w_down": nrm(ks[7], (D_FF, D), D_FF),
        "norm2_g": gain(ks[8], D),
        "w_in": nrm(ks[9], (D, IN_COLS), D),
        "forget_bias": 3.0 + 3.0 * jax.random.uniform(ks[10], (N_ATT_HEADS,), f32),
        "conv_w": nrm(ks[11], (CONV_K, CONV_WIDTH), CONV_K),
        "group_norm_g": gain(ks[12], MIX_WIDTH),
        "w_out": nrm(ks[13], (MIX_WIDTH, D), MIX_WIDTH),
        "norm3_g": gain(ks[14], D),
        "ffn2_w_gate": nrm(ks[15], (D, D_FF), D),
        "ffn2_w_up": nrm(ks[16], (D, D_FF), D),
        "ffn2_w_down": nrm(ks[17], (D_FF, D), D_FF),
        "final_g": gain(ks[18], D),
    }


def reference(x, c, ada_w, ada_b, norm1_g, ffn1_w_gate, ffn1_w_up, ffn1_w_down,
              norm2_g, w_in, forget_bias, conv_w, group_norm_g, w_out,
              norm3_g, ffn2_w_gate, ffn2_w_up, ffn2_w_down, final_g):
    mod = jax.nn.silu(c) @ ada_w + ada_b
    (sh1, sc1, g1, sh2, sc2, g2, sh3, sc3, g3) = jnp.split(mod, N_MOD, axis=-1)
    for _ in range(DEPTH):
        h = modulate(rmsnorm(x, norm1_g), sh1, sc1)
        x = x + 0.5 * g1[:, None, :] * swiglu(h, ffn1_w_gate, ffn1_w_up, ffn1_w_down)
        h = modulate(rmsnorm(x, norm2_g), sh2, sc2)
        x = x + g2[:, None, :] * hybrid_mixer(h, w_in, forget_bias, conv_w, group_norm_g, w_out)
        h = modulate(rmsnorm(x, norm3_g), sh3, sc3)
        x = x + 0.5 * g3[:, None, :] * swiglu(h, ffn2_w_gate, ffn2_w_up, ffn2_w_down)
    return rmsnorm(x, final_g)
```

```python
import functools

import numpy as np
import jax
import jax.numpy as jnp
from jax import lax
from jax.experimental import pallas as pl
from jax.experimental.pallas import tpu as pltpu

HEAD_DIM = 64
N_ATT_HEADS = 8
ATT_WIDTH = N_ATT_HEADS * HEAD_DIM
CONV_K = 3
N_MOD = 9
EPS = 1e-6

LANES = 128
HEADS_PER_BLOCK = LANES // HEAD_DIM
EXTRA_PER_HEAD = 8
VMEM_LIMIT = 56 * 1024 * 1024

FFN_TM = 512
FFN_TF = 256
PROJ_TM = 512
ATT_TQ = 256
ATT_TK = 256
NEG = -1e30

bf16 = jnp.bfloat16
f32 = jnp.float32


def _dot(a, b):
    return jnp.dot(a, b, preferred_element_type=f32)


def _split3(x):
    hi = x.astype(bf16)
    r = x - hi.astype(f32)
    mid = r.astype(bf16)
    lo = (r - mid.astype(f32)).astype(bf16)
    return hi, mid, lo


def _norm_modulate(x, g, shift, scale):
    ms = jnp.mean(x * x, axis=-1, keepdims=True)
    y = x * lax.rsqrt(ms + EPS) * g
    return y * (1.0 + scale) + shift


def _resident(shape):
    return pl.BlockSpec(shape, lambda *_: (0,) * len(shape), pipeline_mode=pl.Buffered(1))


def _adaln_kernel(c_ref, w_ref, b_ref, o_ref):
    c = c_ref[...]
    sc = (c * (1.0 / (1.0 + jnp.exp(-c)))).astype(bf16)
    o_ref[...] = _dot(sc, w_ref[...].astype(bf16)) + b_ref[...]


def _adaln(c, ada_w, ada_b):
    B, D = c.shape
    n = ada_w.shape[1]
    return pl.pallas_call(
        _adaln_kernel,
        grid=(n // D,),
        in_specs=[pl.BlockSpec((B, D), lambda j: (0, 0)),
                  pl.BlockSpec((D, D), lambda j: (0, j)),
                  pl.BlockSpec((1, D), lambda j: (0, j))],
        out_specs=pl.BlockSpec((B, D), lambda j: (0, j)),
        out_shape=jax.ShapeDtypeStruct((B, n), f32),
        compiler_params=pltpu.CompilerParams(dimension_semantics=("arbitrary",),
                                             vmem_limit_bytes=VMEM_LIMIT),
        name="adaln",
    )(c, ada_w, ada_b.reshape(1, n))


def _ffn_kernel(*refs, mod_base, with_mixer, with_final):
    it = iter(refs)
    x_ref, mod_ref, ng_ref, wg_ref, wu_ref, wd_ref = (next(it) for _ in range(6))
    if with_mixer:
        att_ref, cvn_ref, wo_ref = (next(it) for _ in range(3))
    if with_final:
        fg_ref = next(it)
    o_ref = next(it)
    a_ref = next(it)

    x = x_ref[...]
    if with_mixer:
        mix = _dot(att_ref[...], wo_ref[:ATT_WIDTH, :]) + _dot(cvn_ref[...], wo_ref[ATT_WIDTH:, :])
        x = x + mod_ref[0, mod_base - 1:mod_base, :] * mix
    shift = mod_ref[0, mod_base:mod_base + 1, :]
    scale = mod_ref[0, mod_base + 1:mod_base + 2, :]
    gate = mod_ref[0, mod_base + 2:mod_base + 3, :]
    h = _norm_modulate(x, ng_ref[...], shift, scale).astype(bf16)
    d_ff = wg_ref.shape[1]
    for c0 in range(0, d_ff, FFN_TF):
        g = _dot(h, wg_ref[:, c0:c0 + FFN_TF])
        u = _dot(h, wu_ref[:, c0:c0 + FFN_TF])
        a_ref[:, c0:c0 + FFN_TF] = (g * (1.0 / (1.0 + jnp.exp(-g))) * u).astype(bf16)
    y = x + (0.5 * gate) * _dot(a_ref[...], wd_ref[...])
    if with_final:
        ms = jnp.mean(y * y, axis=-1, keepdims=True)
        y = y * lax.rsqrt(ms + EPS) * fg_ref[...]
    o_ref[...] = y


def _ffn(x2d, mod, norm_g, wg, wu, wd, *, mod_base, seq, mixer=None, final_g=None):
    T, D = x2d.shape
    d_ff = wg.shape[1]
    tm = FFN_TM
    blocks_per_seq = seq // tm
    row = lambda i: (i, 0)
    args = [x2d, mod, norm_g.reshape(1, D), wg, wu, wd]
    in_specs = [pl.BlockSpec((tm, D), row),
                pl.BlockSpec((1, N_MOD, D), lambda i: (i // blocks_per_seq, 0, 0)),
                _resident((1, D)), _resident((D, d_ff)), _resident((D, d_ff)), _resident((d_ff, D))]
    if mixer is not None:
        att, cvn, wo = mixer
        args += [att, cvn, wo]
        in_specs += [pl.BlockSpec((tm, att.shape[1]), row), pl.BlockSpec((tm, cvn.shape[1]), row),
                     _resident(wo.shape)]
    if final_g is not None:
        args.append(final_g.reshape(1, D))
        in_specs.append(_resident((1, D)))
    kern = functools.partial(_ffn_kernel, mod_base=mod_base, with_mixer=mixer is not None,
                             with_final=final_g is not None)
    return pl.pallas_call(
        kern,
        grid=(T // tm,),
        in_specs=in_specs,
        out_specs=pl.BlockSpec((tm, D), row),
        out_shape=jax.ShapeDtypeStruct((T, D), f32),
        scratch_shapes=[pltpu.VMEM((tm, d_ff), bf16)],
        compiler_params=pltpu.CompilerParams(dimension_semantics=("arbitrary",),
                                             vmem_limit_bytes=VMEM_LIMIT),
        name="ffn_mix" if mixer is not None else "ffn",
    )(*args)


def _inproj_kernel(x_ref, mod_ref, ng_ref, wqkv_ref, wf_ref, wbcx_ref, fb_ref, cw_ref, gg_ref,
                   tri_ref, pq_ref, pk_ref, oq_ref, ok_ref, gsum_ref, gexp_ref,
                   q_ref, k_ref, v_ref, qe_ref, ke_ref, cvn_ref,
                   fcarry_ref, u_ref):
    j = pl.program_id(1)
    tm = x_ref.shape[1]
    W = ATT_WIDTH
    x = x_ref[0]
    h = _norm_modulate(x, ng_ref[...], mod_ref[0, 3:4, :], mod_ref[0, 4:5, :]).astype(bf16)

    qkv = _dot(h, wqkv_ref[...])
    q_ref[0] = qkv[:, :W].astype(bf16)
    k_ref[0] = qkv[:, W:2 * W].astype(bf16)
    v_ref[0] = qkv[:, 2 * W:].astype(bf16)

    fl = _dot(h, wf_ref[...]) + fb_ref[...]
    lf = jnp.minimum(fl, 0.0) - jnp.log1p(jnp.exp(-jnp.abs(fl)))
    lane = lax.broadcasted_iota(jnp.int32, lf.shape, 1)
    lf = jnp.where(lane < N_ATT_HEADS, lf, 0.0)
    tri = tri_ref[...]
    hi, mid, lo = _split3(lf)
    fblk = _dot(tri, hi) + _dot(tri, mid) + _dot(tri, lo)

    @pl.when(j == 0)
    def _():
        fcarry_ref[...] = jnp.zeros_like(fcarry_ref)

    F = fblk + fcarry_ref[...]
    fcarry_ref[...] = F[tm - 1:tm, :]
    f3 = jnp.concatenate(_split3(F), axis=1)
    qe_ref[0] = (_dot(f3, pq_ref[...]) + oq_ref[...]).astype(bf16)
    ke_ref[0] = (_dot(f3, pk_ref[...]) + ok_ref[...]).astype(bf16)

    bcx = _dot(h, wbcx_ref[...])
    gb = bcx[:, :W]
    u = bcx[:, W:2 * W] * bcx[:, 2 * W:]

    @pl.when(j == 0)
    def _():
        u_ref[0:8, :] = jnp.zeros((8, W), f32)

    @pl.when(j != 0)
    def _():
        u_ref[0:8, :] = u_ref[tm:tm + 8, :]

    u_ref[8:tm + 8, :] = u
    y = cw_ref[0:1, :] * u_ref[6:6 + tm, :] + cw_ref[1:2, :] * u_ref[7:7 + tm, :] + cw_ref[2:3, :] * u
    cv = gb * y

    s_hi, s_mid, s_lo = _split3(cv * cv)
    gsum = gsum_ref[...]
    ss = _dot(s_hi, gsum) + _dot(s_mid, gsum) + _dot(s_lo, gsum)
    inv = lax.rsqrt(ss * (1.0 / HEAD_DIM) + EPS)
    i_hi, i_mid, i_lo = _split3(inv)
    gexp = gexp_ref[...]
    invx = _dot(i_hi, gexp) + _dot(i_mid, gexp) + _dot(i_lo, gexp)
    cvn_ref[0] = (cv * invx * gg_ref[...]).astype(bf16)


def _bias_lane_tables():
    pq = np.zeros((3 * LANES, LANES), np.float32)
    pk = np.zeros((3 * LANES, LANES), np.float32)
    oq = np.zeros((1, LANES), np.float32)
    ok = np.zeros((1, LANES), np.float32)
    for h in range(N_ATT_HEADS):
        for part in range(3):
            pq[part * LANES + h, EXTRA_PER_HEAD * h + part] = 1.0
            pk[part * LANES + h, EXTRA_PER_HEAD * h + 3 + part] = -1.0
            oq[0, EXTRA_PER_HEAD * h + 3 + part] = 1.0
            ok[0, EXTRA_PER_HEAD * h + part] = 1.0
    return pq, pk, oq, ok


def _inproj(x1, mod, norm_g, w_in, forget_bias, conv_w, gn_conv):
    B, S, D = x1.shape
    W = ATT_WIDTH
    tm = PROJ_TM
    scale = 1.0 / np.sqrt(HEAD_DIM)
    wqkv = jnp.concatenate([w_in[:, :W] * scale, w_in[:, W:3 * W]], axis=1).astype(bf16)
    wf = w_in[:, 3 * W:3 * W + LANES].astype(bf16)
    wbcx = w_in[:, 3 * W + N_ATT_HEADS:].astype(bf16)
    fb = jnp.zeros((1, LANES), f32).at[0, :N_ATT_HEADS].set(forget_bias)
    pq, pk, oq, ok = _bias_lane_tables()
    tri = np.tril(np.ones((tm, tm), np.float32))
    group = np.arange(W) // HEAD_DIM
    gsum = (group[:, None] == np.arange(LANES)[None, :]).astype(np.float32)
    blk = lambda b, j: (b, j, 0)
    out_w = lambda w, dt: jax.ShapeDtypeStruct((B, S, w), dt)
    return pl.pallas_call(
        _inproj_kernel,
        grid=(B, S // tm),
        in_specs=[pl.BlockSpec((1, tm, D), blk),
                  pl.BlockSpec((1, N_MOD, D), lambda b, j: (b, 0, 0)),
                  _resident((1, D)), _resident((D, 3 * W)), _resident((D, LANES)), _resident((D, 3 * W)),
                  _resident((1, LANES)), _resident((CONV_K, W)), _resident((1, W)),
                  _resident((tm, tm)), _resident((3 * LANES, LANES)), _resident((3 * LANES, LANES)),
                  _resident((1, LANES)), _resident((1, LANES)), _resident((W, LANES)), _resident((LANES, W))],
        out_specs=[pl.BlockSpec((1, tm, W), blk)] * 3 + [pl.BlockSpec((1, tm, LANES), blk)] * 2
                  + [pl.BlockSpec((1, tm, W), blk)],
        out_shape=[out_w(W, bf16)] * 3 + [out_w(LANES, bf16)] * 2 + [out_w(W, bf16)],
        scratch_shapes=[pltpu.VMEM((1, LANES), f32), pltpu.VMEM((tm + 8, W), f32)],
        compiler_params=pltpu.CompilerParams(dimension_semantics=("arbitrary", "arbitrary"),
                                             vmem_limit_bytes=VMEM_LIMIT),
        name="inproj",
    )(x1, mod, norm_g.reshape(1, D), wqkv, wf, wbcx, fb, conv_w, gn_conv.reshape(1, W),
      jnp.asarray(tri, bf16), jnp.asarray(pq, bf16), jnp.asarray(pk, bf16), jnp.asarray(oq), jnp.asarray(ok),
      jnp.asarray(gsum, bf16), jnp.asarray(gsum.T.copy(), bf16))


def _attention_kernel(q_ref, k_ref, v_ref, qe_ref, ke_ref, gg_ref, o_ref):
    p = pl.program_id(1)
    S = q_ref.shape[1]
    tq, tk = ATT_TQ, ATT_TK
    lane = lax.broadcasted_iota(jnp.int32, (1, LANES), 1)
    first_head = lane < HEAD_DIM
    row2 = lax.broadcasted_iota(jnp.int32, (2 * tq, 1), 0)
    head2 = p * HEADS_PER_BLOCK + (row2 >= tq).astype(jnp.int32)
    keep_q = (lane >= HEAD_DIM) == (row2 >= tq)
    keep_e = (lane >= EXTRA_PER_HEAD * head2) & (lane < EXTRA_PER_HEAD * (head2 + 1))
    qpos = jnp.where(row2 >= tq, row2 - tq, row2)
    kpos = lax.broadcasted_iota(jnp.int32, (1, tk), 1)
    causal = kpos <= qpos
    contract_last = (((1,), (1,)), ((), ()))

    def q_block(i, _):
        q0 = pl.multiple_of(i * tq, tq)
        q = q_ref[0, pl.ds(q0, tq), :]
        qe = qe_ref[0, pl.ds(q0, tq), :]
        q2 = jnp.concatenate([q, q], axis=0)
        qe2 = jnp.concatenate([qe, qe], axis=0)
        qa = jnp.concatenate([jnp.where(keep_q, q2, jnp.zeros_like(q2)),
                              jnp.where(keep_e, qe2, jnp.zeros_like(qe2))], axis=1)

        def scores(kb):
            k0 = pl.multiple_of(kb * tk, tk)
            ka = jnp.concatenate([k_ref[0, pl.ds(k0, tk), :], ke_ref[0, pl.ds(k0, tk), :]], axis=1)
            return lax.dot_general(qa, ka, contract_last, preferred_element_type=f32), k0

        def update(carry, s, k0):
            m, l, acc = carry
            m_new = jnp.maximum(m, jnp.max(s, axis=-1, keepdims=True))
            alpha = jnp.exp(m - m_new)
            pr = jnp.exp(s - m_new)
            l = alpha * l + jnp.sum(pr, axis=-1, keepdims=True)
            acc = alpha * acc + _dot(pr.astype(bf16), v_ref[0, pl.ds(k0, tk), :])
            return m_new, l, acc

        def kv_block(kb, carry):
            s, k0 = scores(kb)
            return update(carry, s, k0)

        init = (jnp.full((2 * tq, 1), NEG, f32), jnp.zeros((2 * tq, 1), f32),
                jnp.zeros((2 * tq, LANES), f32))
        carry = lax.fori_loop(0, i, kv_block, init)
        s, k0 = scores(i)
        m, l, acc = update(carry, jnp.where(causal, s, NEG), k0)

        o2 = acc / l
        o = jnp.where(first_head, o2[:tq], o2[tq:])
        sq = o * o
        ss0 = jnp.sum(jnp.where(first_head, sq, 0.0), axis=-1, keepdims=True)
        ss1 = jnp.sum(jnp.where(first_head, 0.0, sq), axis=-1, keepdims=True)
        ms = jnp.where(first_head, ss0, ss1) * (1.0 / HEAD_DIM)
        o_ref[0, pl.ds(q0, tq), :] = (o * lax.rsqrt(ms + EPS) * gg_ref[...]).astype(bf16)
        return 0

    lax.fori_loop(0, S // tq, q_block, 0)


def _attention(q, k, v, qe, ke, gn_att):
    B, S, W = q.shape
    pair = lambda b, p: (b, 0, p)
    whole = lambda b, p: (b, 0, 0)
    return pl.pallas_call(
        _attention_kernel,
        grid=(B, W // LANES),
        in_specs=[pl.BlockSpec((1, S, LANES), pair)] * 3 + [pl.BlockSpec((1, S, LANES), whole)] * 2
                 + [pl.BlockSpec((1, LANES), lambda b, p: (0, p))],
        out_specs=pl.BlockSpec((1, S, LANES), pair),
        out_shape=jax.ShapeDtypeStruct((B, S, W), bf16),
        compiler_params=pltpu.CompilerParams(dimension_semantics=("arbitrary", "arbitrary"),
                                             vmem_limit_bytes=VMEM_LIMIT),
        name="attention",
    )(q, k, v, qe, ke, gn_att.reshape(1, W))


def kernel(x, c, ada_w, ada_b, norm1_g, ffn1_w_gate, ffn1_w_up, ffn1_w_down, norm2_g, w_in, forget_bias,
           conv_w, group_norm_g, w_out, norm3_g, ffn2_w_gate, ffn2_w_up, ffn2_w_down, final_g):
    B, S, D = x.shape
    assert S % FFN_TM == 0 and S % PROJ_TM == 0 and S % ATT_TQ == 0 and ATT_TQ == ATT_TK
    mod = _adaln(c, ada_w, ada_b).reshape(B, N_MOD, D)
    x1 = _ffn(x.reshape(B * S, D), mod, norm1_g, ffn1_w_gate.astype(bf16), ffn1_w_up.astype(bf16),
              ffn1_w_down.astype(bf16), mod_base=0, seq=S)
    q, k, v, qe, ke, cvn = _inproj(x1.reshape(B, S, D), mod, norm2_g, w_in, forget_bias, conv_w,
                                   group_norm_g[ATT_WIDTH:])
    att = _attention(q, k, v, qe, ke, group_norm_g[:ATT_WIDTH])
    out = _ffn(x1, mod, norm3_g, ffn2_w_gate.astype(bf16), ffn2_w_up.astype(bf16), ffn2_w_down.astype(bf16),
               mod_base=6, seq=S,
               mixer=(att.reshape(B * S, ATT_WIDTH), cvn.reshape(B * S, ATT_WIDTH), w_out.astype(bf16)),
               final_g=final_g)
    return out.reshape(B, S, D)
```

```python
import functools

import numpy as np
import jax
import jax.numpy as jnp
from jax import lax
from jax.experimental import pallas as pl
from jax.experimental.pallas import tpu as pltpu

HEAD_DIM = 64
N_ATT_HEADS = 8
ATT_WIDTH = N_ATT_HEADS * HEAD_DIM
CONV_K = 3
N_MOD = 9
EPS = 1e-6

LANES = 128
HEADS_PER_BLOCK = LANES // HEAD_DIM
EXTRA_PER_HEAD = 8
VMEM_LIMIT = 56 * 1024 * 1024

FFN_TM = 512
FFN_TF = 256
PROJ_TM = 512
ATT_TQ = 512
ATT_TK = 512
NEG = -1e30

bf16 = jnp.bfloat16
f32 = jnp.float32


def _dot(a, b):
    return jnp.dot(a, b, preferred_element_type=f32)


def _split3(x):
    hi = x.astype(bf16)
    r = x - hi.astype(f32)
    mid = r.astype(bf16)
    lo = (r - mid.astype(f32)).astype(bf16)
    return hi, mid, lo


def _norm_modulate(x, g, shift, scale):
    ms = jnp.mean(x * x, axis=-1, keepdims=True)
    y = x * lax.rsqrt(ms + EPS) * g
    return y * (1.0 + scale) + shift


def _resident(shape):
    return pl.BlockSpec(shape, lambda *_: (0,) * len(shape), pipeline_mode=pl.Buffered(1))


def _adaln_kernel(c_ref, w_ref, b_ref, o_ref):
    c = c_ref[...]
    sc = (c * (1.0 / (1.0 + jnp.exp(-c)))).astype(bf16)
    o_ref[...] = _dot(sc, w_ref[...].astype(bf16)) + b_ref[...]


def _adaln(c, ada_w, ada_b):
    B, D = c.shape
    n = ada_w.shape[1]
    return pl.pallas_call(
        _adaln_kernel,
        grid=(n // D,),
        in_specs=[pl.BlockSpec((B, D), lambda j: (0, 0)),
                  pl.BlockSpec((D, D), lambda j: (0, j)),
                  pl.BlockSpec((1, D), lambda j: (0, j))],
        out_specs=pl.BlockSpec((B, D), lambda j: (0, j)),
        out_shape=jax.ShapeDtypeStruct((B, n), f32),
        compiler_params=pltpu.CompilerParams(dimension_semantics=("arbitrary",),
                                             vmem_limit_bytes=VMEM_LIMIT),
        name="adaln",
    )(c, ada_w, ada_b.reshape(1, n))


def _ffn_kernel(*refs, mod_base, with_mixer, with_final):
    it = iter(refs)
    x_ref, mod_ref, ng_ref, wg_ref, wu_ref, wd_ref = (next(it) for _ in range(6))
    if with_mixer:
        att_ref, cvn_ref, wo_ref = (next(it) for _ in range(3))
    if with_final:
        fg_ref = next(it)
    o_ref = next(it)
    a_ref = next(it)

    x = x_ref[...]
    if with_mixer:
        mix = _dot(att_ref[...], wo_ref[:ATT_WIDTH, :]) + _dot(cvn_ref[...], wo_ref[ATT_WIDTH:, :])
        x = x + mod_ref[0, mod_base - 1:mod_base, :] * mix
    shift = mod_ref[0, mod_base:mod_base + 1, :]
    scale = mod_ref[0, mod_base + 1:mod_base + 2, :]
    gate = mod_ref[0, mod_base + 2:mod_base + 3, :]
    h = _norm_modulate(x, ng_ref[...], shift, scale).astype(bf16)
    d_ff = wg_ref.shape[1]
    for c0 in range(0, d_ff, FFN_TF):
        g = _dot(h, wg_ref[:, c0:c0 + FFN_TF])
        u = _dot(h, wu_ref[:, c0:c0 + FFN_TF])
        a_ref[:, c0:c0 + FFN_TF] = (g * (1.0 / (1.0 + jnp.exp(-g))) * u).astype(bf16)
    y = x + (0.5 * gate) * _dot(a_ref[...], wd_ref[...])
    if with_final:
        ms = jnp.mean(y * y, axis=-1, keepdims=True)
        y = y * lax.rsqrt(ms + EPS) * fg_ref[...]
    o_ref[...] = y


def _ffn(x2d, mod, norm_g, wg, wu, wd, *, mod_base, seq, mixer=None, final_g=None):
    T, D = x2d.shape
    d_ff = wg.shape[1]
    tm = FFN_TM
    blocks_per_seq = seq // tm
    row = lambda i: (i, 0)
    args = [x2d, mod, norm_g.reshape(1, D), wg, wu, wd]
    in_specs = [pl.BlockSpec((tm, D), row),
                pl.BlockSpec((1, N_MOD, D), lambda i: (i // blocks_per_seq, 0, 0)),
                _resident((1, D)), _resident((D, d_ff)), _resident((D, d_ff)), _resident((d_ff, D))]
    if mixer is not None:
        att, cvn, wo = mixer
        args += [att, cvn, wo]
        in_specs += [pl.BlockSpec((tm, att.shape[1]), row), pl.BlockSpec((tm, cvn.shape[1]), row),
                     _resident(wo.shape)]
    if final_g is not None:
        args.append(final_g.reshape(1, D))
        in_specs.append(_resident((1, D)))
    kern = functools.partial(_ffn_kernel, mod_base=mod_base, with_mixer=mixer is not None,
                             with_final=final_g is not None)
    return pl.pallas_call(
        kern,
        grid=(T // tm,),
        in_specs=in_specs,
        out_specs=pl.BlockSpec((tm, D), row),
        out_shape=jax.ShapeDtypeStruct((T, D), f32),
        scratch_shapes=[pltpu.VMEM((tm, d_ff), bf16)],
        compiler_params=pltpu.CompilerParams(dimension_semantics=("arbitrary",),
                                             vmem_limit_bytes=VMEM_LIMIT),
        name="ffn_mix" if mixer is not None else "ffn",
    )(*args)


def _inproj_kernel(x_ref, mod_ref, ng_ref, wqkv_ref, wf_ref, wbcx_ref, fb_ref, cw_ref, gg_ref,
                   tri_ref, pq_ref, pk_ref, oq_ref, ok_ref, gsum_ref, gexp_ref,
                   q_ref, k_ref, v_ref, qe_ref, ke_ref, cvn_ref,
                   fcarry_ref, u_ref):
    j = pl.program_id(1)
    tm = x_ref.shape[1]
    W = ATT_WIDTH
    x = x_ref[0]
    h = _norm_modulate(x, ng_ref[...], mod_ref[0, 3:4, :], mod_ref[0, 4:5, :]).astype(bf16)

    qkv = _dot(h, wqkv_ref[...])
    q_ref[0] = qkv[:, :W].astype(bf16)
    k_ref[0] = qkv[:, W:2 * W].astype(bf16)
    v_ref[0] = qkv[:, 2 * W:].astype(bf16)

    fl = _dot(h, wf_ref[...]) + fb_ref[...]
    lf = jnp.minimum(fl, 0.0) - jnp.log1p(jnp.exp(-jnp.abs(fl)))
    lane = lax.broadcasted_iota(jnp.int32, lf.shape, 1)
    lf = jnp.where(lane < N_ATT_HEADS, lf, 0.0)
    tri = tri_ref[...]
    hi, mid, lo = _split3(lf)
    fblk = _dot(tri, hi) + _dot(tri, mid) + _dot(tri, lo)

    @pl.when(j == 0)
    def _():
        fcarry_ref[...] = jnp.zeros_like(fcarry_ref)

    F = fblk + fcarry_ref[...]
    fcarry_ref[...] = F[tm - 1:tm, :]
    f3 = jnp.concatenate(_split3(F), axis=1)
    qe_ref[0] = (_dot(f3, pq_ref[...]) + oq_ref[...]).astype(bf16)
    ke_ref[0] = (_dot(f3, pk_ref[...]) + ok_ref[...]).astype(bf16)

    bcx = _dot(h, wbcx_ref[...])
    gb = bcx[:, :W]
    u = bcx[:, W:2 * W] * bcx[:, 2 * W:]

    @pl.when(j == 0)
    def _():
        u_ref[0:8, :] = jnp.zeros((8, W), f32)

    @pl.when(j != 0)
    def _():
        u_ref[0:8, :] = u_ref[tm:tm + 8, :]

    u_ref[8:tm + 8, :] = u
    y = cw_ref[0:1, :] * u_ref[6:6 + tm, :] + cw_ref[1:2, :] * u_ref[7:7 + tm, :] + cw_ref[2:3, :] * u
    cv = gb * y

    s_hi, s_mid, s_lo = _split3(cv * cv)
    gsum = gsum_ref[...]
    ss = _dot(s_hi, gsum) + _dot(s_mid, gsum) + _dot(s_lo, gsum)
    inv = lax.rsqrt(ss * (1.0 / HEAD_DIM) + EPS)
    i_hi, i_mid, i_lo = _split3(inv)
    gexp = gexp_ref[...]
    invx = _dot(i_hi, gexp) + _dot(i_mid, gexp) + _dot(i_lo, gexp)
    cvn_ref[0] = (cv * invx * gg_ref[...]).astype(bf16)


def _bias_lane_tables():
    pq = np.zeros((3 * LANES, LANES), np.float32)
    pk = np.zeros((3 * LANES, LANES), np.float32)
    oq = np.zeros((1, LANES), np.float32)
    ok = np.zeros((1, LANES), np.float32)
    for h in range(N_ATT_HEADS):
        for part in range(3):
            pq[part * LANES + h, EXTRA_PER_HEAD * h + part] = 1.0
            pk[part * LANES + h, EXTRA_PER_HEAD * h + 3 + part] = -1.0
            oq[0, EXTRA_PER_HEAD * h + 3 + part] = 1.0
            ok[0, EXTRA_PER_HEAD * h + part] = 1.0
    return pq, pk, oq, ok


def _inproj(x1, mod, norm_g, w_in, forget_bias, conv_w, gn_conv):
    B, S, D = x1.shape
    W = ATT_WIDTH
    tm = PROJ_TM
    scale = 1.0 / np.sqrt(HEAD_DIM)
    wqkv = jnp.concatenate([w_in[:, :W] * scale, w_in[:, W:3 * W]], axis=1).astype(bf16)
    wf = w_in[:, 3 * W:3 * W + LANES].astype(bf16)
    wbcx = w_in[:, 3 * W + N_ATT_HEADS:].astype(bf16)
    fb = jnp.zeros((1, LANES), f32).at[0, :N_ATT_HEADS].set(forget_bias)
    pq, pk, oq, ok = _bias_lane_tables()
    tri = np.tril(np.ones((tm, tm), np.float32))
    group = np.arange(W) // HEAD_DIM
    gsum = (group[:, None] == np.arange(LANES)[None, :]).astype(np.float32)
    blk = lambda b, j: (b, j, 0)
    out_w = lambda w, dt: jax.ShapeDtypeStruct((B, S, w), dt)
    return pl.pallas_call(
        _inproj_kernel,
        grid=(B, S // tm),
        in_specs=[pl.BlockSpec((1, tm, D), blk),
                  pl.BlockSpec((1, N_MOD, D), lambda b, j: (b, 0, 0)),
                  _resident((1, D)), _resident((D, 3 * W)), _resident((D, LANES)), _resident((D, 3 * W)),
                  _resident((1, LANES)), _resident((CONV_K, W)), _resident((1, W)),
                  _resident((tm, tm)), _resident((3 * LANES, LANES)), _resident((3 * LANES, LANES)),
                  _resident((1, LANES)), _resident((1, LANES)), _resident((W, LANES)), _resident((LANES, W))],
        out_specs=[pl.BlockSpec((1, tm, W), blk)] * 3 + [pl.BlockSpec((1, tm, LANES), blk)] * 2
                  + [pl.BlockSpec((1, tm, W), blk)],
        out_shape=[out_w(W, bf16)] * 3 + [out_w(LANES, bf16)] * 2 + [out_w(W, bf16)],
        scratch_shapes=[pltpu.VMEM((1, LANES), f32), pltpu.VMEM((tm + 8, W), f32)],
        compiler_params=pltpu.CompilerParams(dimension_semantics=("arbitrary", "arbitrary"),
                                             vmem_limit_bytes=VMEM_LIMIT),
        name="inproj",
    )(x1, mod, norm_g.reshape(1, D), wqkv, wf, wbcx, fb, conv_w, gn_conv.reshape(1, W),
      jnp.asarray(tri, bf16), jnp.asarray(pq, bf16), jnp.asarray(pk, bf16), jnp.asarray(oq), jnp.asarray(ok),
      jnp.asarray(gsum, bf16), jnp.asarray(gsum.T.copy(), bf16))


def _attention_kernel(q_ref, k_ref, v_ref, qe_ref, ke_ref, gg_ref, o_ref):
    p = pl.program_id(1)
    S = q_ref.shape[1]
    tq, tk = ATT_TQ, ATT_TK
    lane = lax.broadcasted_iota(jnp.int32, (1, LANES), 1)
    first_head = lane < HEAD_DIM
    row2 = lax.broadcasted_iota(jnp.int32, (2 * tq, 1), 0)
    head2 = p * HEADS_PER_BLOCK + (row2 >= tq).astype(jnp.int32)
    keep_q = (lane >= HEAD_DIM) == (row2 >= tq)
    keep_e = (lane >= EXTRA_PER_HEAD * head2) & (lane < EXTRA_PER_HEAD * (head2 + 1))
    qpos = jnp.where(row2 >= tq, row2 - tq, row2)
    kpos = lax.broadcasted_iota(jnp.int32, (1, tk), 1)
    causal = kpos <= qpos
    contract_last = (((1,), (1,)), ((), ()))

    def q_block(i, _):
        q0 = pl.multiple_of(i * tq, tq)
        q = q_ref[0, pl.ds(q0, tq), :]
        qe = qe_ref[0, pl.ds(q0, tq), :]
        q2 = jnp.concatenate([q, q], axis=0)
        qe2 = jnp.concatenate([qe, qe], axis=0)
        qa = jnp.concatenate([jnp.where(keep_q, q2, jnp.zeros_like(q2)),
                              jnp.where(keep_e, qe2, jnp.zeros_like(qe2))], axis=1)

        def scores(kb):
            k0 = pl.multiple_of(kb * tk, tk)
            ka = jnp.concatenate([k_ref[0, pl.ds(k0, tk), :], ke_ref[0, pl.ds(k0, tk), :]], axis=1)
            return lax.dot_general(qa, ka, contract_last, preferred_element_type=f32), k0

        def update(carry, s, k0):
            m, l, acc = carry
            m_new = jnp.maximum(m, jnp.max(s, axis=-1, keepdims=True))
            alpha = jnp.exp(m - m_new)
            pr = jnp.exp(s - m_new)
            l = alpha * l + jnp.sum(pr, axis=-1, keepdims=True)
            acc = alpha * acc + _dot(pr.astype(bf16), v_ref[0, pl.ds(k0, tk), :])
            return m_new, l, acc

        def kv_block(kb, carry):
            s, k0 = scores(kb)
            return update(carry, s, k0)

        init = (jnp.full((2 * tq, 1), NEG, f32), jnp.zeros((2 * tq, 1), f32),
                jnp.zeros((2 * tq, LANES), f32))
        carry = lax.fori_loop(0, i, kv_block, init)
        s, k0 = scores(i)
        m, l, acc = update(carry, jnp.where(causal, s, NEG), k0)

        o2 = acc / l
        o = jnp.where(first_head, o2[:tq], o2[tq:])
        sq = o * o
        ss0 = jnp.sum(jnp.where(first_head, sq, 0.0), axis=-1, keepdims=True)
        ss1 = jnp.sum(jnp.where(first_head, 0.0, sq), axis=-1, keepdims=True)
        ms = jnp.where(first_head, ss0, ss1) * (1.0 / HEAD_DIM)
        o_ref[0, pl.ds(q0, tq), :] = (o * lax.rsqrt(ms + EPS) * gg_ref[...]).astype(bf16)
        return 0

    lax.fori_loop(0, S // tq, q_block, 0)


def _attention(q, k, v, qe, ke, gn_att):
    B, S, W = q.shape
    pair = lambda b, p: (b, 0, p)
    whole = lambda b, p: (b, 0, 0)
    return pl.pallas_call(
        _attention_kernel,
        grid=(B, W // LANES),
        in_specs=[pl.BlockSpec((1, S, LANES), pair)] * 3 + [pl.BlockSpec((1, S, LANES), whole)] * 2
                 + [pl.BlockSpec((1, LANES), lambda b, p: (0, p))],
        out_specs=pl.BlockSpec((1, S, LANES), pair),
        out_shape=jax.ShapeDtypeStruct((B, S, W), bf16),
        compiler_params=pltpu.CompilerParams(dimension_semantics=("arbitrary", "arbitrary"),
                                             vmem_limit_bytes=VMEM_LIMIT),
        name="attention",
    )(q, k, v, qe, ke, gn_att.reshape(1, W))


def kernel(x, c, ada_w, ada_b, norm1_g, ffn1_w_gate, ffn1_w_up, ffn1_w_down, norm2_g, w_in, forget_bias,
           conv_w, group_norm_g, w_out, norm3_g, ffn2_w_gate, ffn2_w_up, ffn2_w_down, final_g):
    B, S, D = x.shape
    assert S % FFN_TM == 0 and S % PROJ_TM == 0 and S % ATT_TQ == 0 and ATT_TQ == ATT_TK
    mod = _adaln(c, ada_w, ada_b).reshape(B, N_MOD, D)
    x1 = _ffn(x.reshape(B * S, D), mod, norm1_g, ffn1_w_gate.astype(bf16), ffn1_w_up.astype(bf16),
              ffn1_w_down.astype(bf16), mod_base=0, seq=S)
    q, k, v, qe, ke, cvn = _inproj(x1.reshape(B, S, D), mod, norm2_g, w_in, forget_bias, conv_w,
                                   group_norm_g[ATT_WIDTH:])
    att = _attention(q, k, v, qe, ke, group_norm_g[:ATT_WIDTH])
    out = _ffn(x1, mod, norm3_g, ffn2_w_gate.astype(bf16), ffn2_w_up.astype(bf16), ffn2_w_down.astype(bf16),
               mod_base=6, seq=S,
               mixer=(att.reshape(B * S, ATT_WIDTH), cvn.reshape(B * S, ATT_WIDTH), w_out.astype(bf16)),
               final_g=final_g)
    return out.reshape(B, S, D)
```

```python
import functools

import numpy as np
import jax
import jax.numpy as jnp
from jax import lax
from jax.experimental import pallas as pl
from jax.experimental.pallas import tpu as pltpu

HEAD_DIM = 64
N_ATT_HEADS = 8
ATT_WIDTH = N_ATT_HEADS * HEAD_DIM
CONV_K = 3
N_MOD = 9
EPS = 1e-6

LANES = 128
HEADS_PER_BLOCK = LANES // HEAD_DIM
EXTRA_PER_HEAD = 8
VMEM_LIMIT = 56 * 1024 * 1024

FFN_TM = 512
FFN_TF = 256
PROJ_TM = 512
ATT_TQ = 512
ATT_TK = 512
ATT_RC = 64
ATT_STREAMS = 2
NEG = -1e30
LOG2E = float(np.log2(np.e))

bf16 = jnp.bfloat16
f32 = jnp.float32


def _dot(a, b):
    return jnp.dot(a, b, preferred_element_type=f32)


def _split3(x):
    hi = x.astype(bf16)
    r = x - hi.astype(f32)
    mid = r.astype(bf16)
    lo = (r - mid.astype(f32)).astype(bf16)
    return hi, mid, lo


def _norm_modulate(x, g, shift, scale):
    ms = jnp.mean(x * x, axis=-1, keepdims=True)
    y = x * lax.rsqrt(ms + EPS) * g
    return y * (1.0 + scale) + shift


def _resident(shape):
    return pl.BlockSpec(shape, lambda *_: (0,) * len(shape), pipeline_mode=pl.Buffered(1))


def _adaln_kernel(c_ref, w_ref, b_ref, o_ref):
    c = c_ref[...]
    sc = (c * (1.0 / (1.0 + jnp.exp(-c)))).astype(bf16)
    o_ref[...] = _dot(sc, w_ref[...].astype(bf16)) + b_ref[...]


def _adaln(c, ada_w, ada_b):
    B, D = c.shape
    n = ada_w.shape[1]
    return pl.pallas_call(
        _adaln_kernel,
        grid=(n // D,),
        in_specs=[pl.BlockSpec((B, D), lambda j: (0, 0)),
                  pl.BlockSpec((D, D), lambda j: (0, j)),
                  pl.BlockSpec((1, D), lambda j: (0, j))],
        out_specs=pl.BlockSpec((B, D), lambda j: (0, j)),
        out_shape=jax.ShapeDtypeStruct((B, n), f32),
        compiler_params=pltpu.CompilerParams(dimension_semantics=("arbitrary",),
                                             vmem_limit_bytes=VMEM_LIMIT),
        name="adaln",
    )(c, ada_w, ada_b.reshape(1, n))


def _ffn_kernel(*refs, mod_base, with_mixer, with_final):
    it = iter(refs)
    x_ref, mod_ref, ng_ref, wg_ref, wu_ref, wd_ref = (next(it) for _ in range(6))
    if with_mixer:
        att_ref, cvn_ref, wo_ref = (next(it) for _ in range(3))
    if with_final:
        fg_ref = next(it)
    o_ref = next(it)
    a_ref = next(it)

    x = x_ref[...]
    if with_mixer:
        mix = _dot(att_ref[...], wo_ref[:ATT_WIDTH, :]) + _dot(cvn_ref[...], wo_ref[ATT_WIDTH:, :])
        x = x + mod_ref[0, mod_base - 1:mod_base, :] * mix
    shift = mod_ref[0, mod_base:mod_base + 1, :]
    scale = mod_ref[0, mod_base + 1:mod_base + 2, :]
    gate = mod_ref[0, mod_base + 2:mod_base + 3, :]
    h = _norm_modulate(x, ng_ref[...], shift, scale).astype(bf16)
    d_ff = wg_ref.shape[1]
    for c0 in range(0, d_ff, FFN_TF):
        g = _dot(h, wg_ref[:, c0:c0 + FFN_TF])
        u = _dot(h, wu_ref[:, c0:c0 + FFN_TF])
        a_ref[:, c0:c0 + FFN_TF] = (g * (1.0 / (1.0 + jnp.exp(-g))) * u).astype(bf16)
    y = x + (0.5 * gate) * _dot(a_ref[...], wd_ref[...])
    if with_final:
        ms = jnp.mean(y * y, axis=-1, keepdims=True)
        y = y * lax.rsqrt(ms + EPS) * fg_ref[...]
    o_ref[...] = y


def _ffn(x2d, mod, norm_g, wg, wu, wd, *, mod_base, seq, mixer=None, final_g=None):
    T, D = x2d.shape
    d_ff = wg.shape[1]
    tm = FFN_TM
    blocks_per_seq = seq // tm
    row = lambda i: (i, 0)
    args = [x2d, mod, norm_g.reshape(1, D), wg, wu, wd]
    in_specs = [pl.BlockSpec((tm, D), row),
                pl.BlockSpec((1, N_MOD, D), lambda i: (i // blocks_per_seq, 0, 0)),
                _resident((1, D)), _resident((D, d_ff)), _resident((D, d_ff)), _resident((d_ff, D))]
    if mixer is not None:
        att, cvn, wo = mixer
        args += [att, cvn, wo]
        in_specs += [pl.BlockSpec((tm, att.shape[1]), row), pl.BlockSpec((tm, cvn.shape[1]), row),
                     _resident(wo.shape)]
    if final_g is not None:
        args.append(final_g.reshape(1, D))
        in_specs.append(_resident((1, D)))
    kern = functools.partial(_ffn_kernel, mod_base=mod_base, with_mixer=mixer is not None,
                             with_final=final_g is not None)
    return pl.pallas_call(
        kern,
        grid=(T // tm,),
        in_specs=in_specs,
        out_specs=pl.BlockSpec((tm, D), row),
        out_shape=jax.ShapeDtypeStruct((T, D), f32),
        scratch_shapes=[pltpu.VMEM((tm, d_ff), bf16)],
        compiler_params=pltpu.CompilerParams(dimension_semantics=("arbitrary",),
                                             vmem_limit_bytes=VMEM_LIMIT),
        name="ffn_mix" if mixer is not None else "ffn",
    )(*args)


def _inproj_kernel(x_ref, mod_ref, ng_ref, wqkv_ref, wf_ref, wbcx_ref, fb_ref, cw_ref, gg_ref,
                   tri_ref, pq_ref, pk_ref, oq_ref, ok_ref, gsum_ref, gexp_ref,
                   q_ref, k_ref, v_ref, qe_ref, ke_ref, cvn_ref,
                   fcarry_ref, u_ref):
    j = pl.program_id(1)
    tm = x_ref.shape[1]
    W = ATT_WIDTH
    x = x_ref[0]
    h = _norm_modulate(x, ng_ref[...], mod_ref[0, 3:4, :], mod_ref[0, 4:5, :]).astype(bf16)

    qkv = _dot(h, wqkv_ref[...])
    q_ref[0] = qkv[:, :W].astype(bf16)
    k_ref[0] = qkv[:, W:2 * W].astype(bf16)
    v_ref[0] = qkv[:, 2 * W:].astype(bf16)

    fl = _dot(h, wf_ref[...]) + fb_ref[...]
    lf = jnp.minimum(fl, 0.0) - jnp.log1p(jnp.exp(-jnp.abs(fl)))
    lane = lax.broadcasted_iota(jnp.int32, lf.shape, 1)
    lf = jnp.where(lane < N_ATT_HEADS, lf * LOG2E, 0.0)
    tri = tri_ref[...]
    hi, mid, lo = _split3(lf)
    fblk = _dot(tri, hi) + _dot(tri, mid) + _dot(tri, lo)

    @pl.when(j == 0)
    def _():
        fcarry_ref[...] = jnp.zeros_like(fcarry_ref)

    F = fblk + fcarry_ref[...]
    fcarry_ref[...] = F[tm - 1:tm, :]
    f3 = jnp.concatenate(_split3(F), axis=1)
    qe_ref[0] = (_dot(f3, pq_ref[...]) + oq_ref[...]).astype(bf16)
    ke_ref[0] = (_dot(f3, pk_ref[...]) + ok_ref[...]).astype(bf16)

    bcx = _dot(h, wbcx_ref[...])
    gb = bcx[:, :W]
    u = bcx[:, W:2 * W] * bcx[:, 2 * W:]

    @pl.when(j == 0)
    def _():
        u_ref[0:8, :] = jnp.zeros((8, W), f32)

    @pl.when(j != 0)
    def _():
        u_ref[0:8, :] = u_ref[tm:tm + 8, :]

    u_ref[8:tm + 8, :] = u
    y = cw_ref[0:1, :] * u_ref[6:6 + tm, :] + cw_ref[1:2, :] * u_ref[7:7 + tm, :] + cw_ref[2:3, :] * u
    cv = gb * y

    s_hi, s_mid, s_lo = _split3(cv * cv)
    gsum = gsum_ref[...]
    ss = _dot(s_hi, gsum) + _dot(s_mid, gsum) + _dot(s_lo, gsum)
    inv = lax.rsqrt(ss * (1.0 / HEAD_DIM) + EPS)
    i_hi, i_mid, i_lo = _split3(inv)
    gexp = gexp_ref[...]
    invx = _dot(i_hi, gexp) + _dot(i_mid, gexp) + _dot(i_lo, gexp)
    cvn_ref[0] = (cv * invx * gg_ref[...]).astype(bf16)


def _bias_lane_tables():
    pq = np.zeros((3 * LANES, LANES), np.float32)
    pk = np.zeros((3 * LANES, LANES), np.float32)
    oq = np.zeros((1, LANES), np.float32)
    ok = np.zeros((1, LANES), np.float32)
    for h in range(N_ATT_HEADS):
        for part in range(3):
            pq[part * LANES + h, EXTRA_PER_HEAD * h + part] = 1.0
            pk[part * LANES + h, EXTRA_PER_HEAD * h + 3 + part] = -1.0
            oq[0, EXTRA_PER_HEAD * h + 3 + part] = 1.0
            ok[0, EXTRA_PER_HEAD * h + part] = 1.0
    return pq, pk, oq, ok


def _inproj(x1, mod, norm_g, w_in, forget_bias, conv_w, gn_conv):
    B, S, D = x1.shape
    W = ATT_WIDTH
    tm = PROJ_TM
    scale = LOG2E / np.sqrt(HEAD_DIM)
    wqkv = jnp.concatenate([w_in[:, :W] * scale, w_in[:, W:3 * W]], axis=1).astype(bf16)
    wf = w_in[:, 3 * W:3 * W + LANES].astype(bf16)
    wbcx = w_in[:, 3 * W + N_ATT_HEADS:].astype(bf16)
    fb = jnp.zeros((1, LANES), f32).at[0, :N_ATT_HEADS].set(forget_bias)
    pq, pk, oq, ok = _bias_lane_tables()
    tri = np.tril(np.ones((tm, tm), np.float32))
    group = np.arange(W) // HEAD_DIM
    gsum = (group[:, None] == np.arange(LANES)[None, :]).astype(np.float32)
    blk = lambda b, j: (b, j, 0)
    out_w = lambda w, dt: jax.ShapeDtypeStruct((B, S, w), dt)
    return pl.pallas_call(
        _inproj_kernel,
        grid=(B, S // tm),
        in_specs=[pl.BlockSpec((1, tm, D), blk),
                  pl.BlockSpec((1, N_MOD, D), lambda b, j: (b, 0, 0)),
                  _resident((1, D)), _resident((D, 3 * W)), _resident((D, LANES)), _resident((D, 3 * W)),
                  _resident((1, LANES)), _resident((CONV_K, W)), _resident((1, W)),
                  _resident((tm, tm)), _resident((3 * LANES, LANES)), _resident((3 * LANES, LANES)),
                  _resident((1, LANES)), _resident((1, LANES)), _resident((W, LANES)), _resident((LANES, W))],
        out_specs=[pl.BlockSpec((1, tm, W), blk)] * 3 + [pl.BlockSpec((1, tm, LANES), blk)] * 2
                  + [pl.BlockSpec((1, tm, W), blk)],
        out_shape=[out_w(W, bf16)] * 3 + [out_w(LANES, bf16)] * 2 + [out_w(W, bf16)],
        scratch_shapes=[pltpu.VMEM((1, LANES), f32), pltpu.VMEM((tm + 8, W), f32)],
        compiler_params=pltpu.CompilerParams(dimension_semantics=("arbitrary", "arbitrary"),
                                             vmem_limit_bytes=VMEM_LIMIT),
        name="inproj",
    )(x1, mod, norm_g.reshape(1, D), wqkv, wf, wbcx, fb, conv_w, gn_conv.reshape(1, W),
      jnp.asarray(tri, bf16), jnp.asarray(pq, bf16), jnp.asarray(pk, bf16), jnp.asarray(oq), jnp.asarray(ok),
      jnp.asarray(gsum, bf16), jnp.asarray(gsum.T.copy(), bf16))


def _attention_kernel(q_ref, k_ref, v_ref, qe_ref, ke_ref, gg_ref, o_ref,
                      qa_scr, s_scr, p_scr, m_scr, l_scr, a_scr, acc_scr):
    S = q_ref.shape[1]
    tq, tk, rc = ATT_TQ, ATT_TK, ATT_RC
    R = HEADS_PER_BLOCK * tq
    streams = range(ATT_STREAMS)
    lane = lax.broadcasted_iota(jnp.int32, (1, LANES), 1)
    first_head = lane < HEAD_DIM
    contract_last = (((1,), (1,)), ((), ()))

    def softmax_block(t, diagonal):
        for r0 in range(0, R, rc):
            rq = r0 % tq
            rows = slice(r0, r0 + rc)
            visible, hidden = [], []
            for c0 in range(0, tk, LANES):
                cols = slice(c0, c0 + LANES)
                if diagonal and c0 > rq + rc - 1:
                    hidden.append(cols)
                    continue
                s = s_scr[t, rows, cols]
                if diagonal and c0 + LANES - 1 > rq:
                    qpos = rq + lax.broadcasted_iota(jnp.int32, (rc, LANES), 0)
                    kpos = c0 + lax.broadcasted_iota(jnp.int32, (rc, LANES), 1)
                    s = jnp.where(kpos <= qpos, s, NEG)
                visible.append((cols, s))
            mx = visible[0][1]
            for _, s in visible[1:]:
                mx = jnp.maximum(mx, s)
            m_prev = m_scr[t, rows, :]
            m_new = jnp.maximum(m_prev, jnp.max(mx, axis=-1, keepdims=True))
            alpha = jnp.exp2(m_prev - m_new)
            tot = None
            for cols, s in visible:
                pr = jnp.exp2(s - m_new)
                tot = pr if tot is None else tot + pr
                p_scr[t, rows, cols] = pr.astype(bf16)
            for cols in hidden:
                p_scr[t, rows, cols] = jnp.zeros((rc, LANES), bf16)
            l_scr[t, rows, :] = alpha * l_scr[t, rows, :] + tot
            m_scr[t, rows, :] = m_new
            a_scr[t, rows, :] = alpha

    def kv_step(kb, diagonal):
        k0 = pl.multiple_of(kb * tk, tk)
        ke = ke_ref[0, pl.ds(k0, tk), :]
        for t in streams:
            ka = jnp.concatenate([k_ref[0, pl.ds(k0, tk), t * LANES:(t + 1) * LANES], ke], axis=1)
            s_scr[t] = lax.dot_general(qa_scr[t], ka, contract_last, preferred_element_type=f32)
        for t in streams:
            softmax_block(t, diagonal)
            pv = _dot(p_scr[t], v_ref[0, pl.ds(k0, tk), t * LANES:(t + 1) * LANES])
            acc_scr[t] = a_scr[t] * acc_scr[t] + pv

    def q_block(i, _):
        q0 = pl.multiple_of(i * tq, tq)
        qe = qe_ref[0, pl.ds(q0, tq), :]
        for t in streams:
            q = q_ref[0, pl.ds(q0, tq), t * LANES:(t + 1) * LANES]
            zq = jnp.zeros_like(q)
            pair = pl.program_id(1) * ATT_STREAMS + t
            for hh in range(HEADS_PER_BLOCK):
                rows = slice(hh * tq, (hh + 1) * tq)
                qa_scr[t, rows, :LANES] = jnp.where((lane >= HEAD_DIM * hh) & (lane < HEAD_DIM * (hh + 1)), q, zq)
                lo = EXTRA_PER_HEAD * (HEADS_PER_BLOCK * pair + hh)
                qa_scr[t, rows, LANES:] = jnp.where((lane >= lo) & (lane < lo + EXTRA_PER_HEAD), qe, zq)
        m_scr[...] = jnp.full(m_scr.shape, NEG, f32)
        l_scr[...] = jnp.zeros(l_scr.shape, f32)
        acc_scr[...] = jnp.zeros(acc_scr.shape, f32)

        def full_step(kb, carry):
            kv_step(kb, False)
            return carry

        lax.fori_loop(0, i, full_step, 0)
        kv_step(i, True)

        for t in streams:
            o2 = acc_scr[t] / jnp.sum(l_scr[t], axis=-1, keepdims=True)
            o = jnp.where(first_head, o2[:tq], o2[tq:])
            sq = o * o
            ss0 = jnp.sum(jnp.where(first_head, sq, 0.0), axis=-1, keepdims=True)
            ss1 = jnp.sum(jnp.where(first_head, 0.0, sq), axis=-1, keepdims=True)
            ms = jnp.where(first_head, ss0, ss1) * (1.0 / HEAD_DIM)
            y = o * lax.rsqrt(ms + EPS) * gg_ref[:, t * LANES:(t + 1) * LANES]
            o_ref[0, pl.ds(q0, tq), t * LANES:(t + 1) * LANES] = y.astype(bf16)
        return 0

    lax.fori_loop(0, S // tq, q_block, 0)


def _attention(q, k, v, qe, ke, gn_att):
    B, S, W = q.shape
    R = HEADS_PER_BLOCK * ATT_TQ
    wb = ATT_STREAMS * LANES
    group = lambda b, p: (b, 0, p)
    whole = lambda b, p: (b, 0, 0)
    per_stream = lambda cols, dt: pltpu.VMEM((ATT_STREAMS, R, cols), dt)
    return pl.pallas_call(
        _attention_kernel,
        grid=(B, W // wb),
        in_specs=[pl.BlockSpec((1, S, wb), group)] * 3 + [pl.BlockSpec((1, S, LANES), whole)] * 2
                 + [pl.BlockSpec((1, wb), lambda b, p: (0, p))],
        out_specs=pl.BlockSpec((1, S, wb), group),
        out_shape=jax.ShapeDtypeStruct((B, S, W), bf16),
        scratch_shapes=[per_stream(2 * LANES, bf16), per_stream(ATT_TK, f32), per_stream(ATT_TK, bf16)]
                       + [per_stream(LANES, f32)] * 4,
        compiler_params=pltpu.CompilerParams(dimension_semantics=("arbitrary", "arbitrary"),
                                             vmem_limit_bytes=VMEM_LIMIT),
        name="attention",
    )(q, k, v, qe, ke, gn_att.reshape(1, W))


def kernel(x, c, ada_w, ada_b, norm1_g, ffn1_w_gate, ffn1_w_up, ffn1_w_down, norm2_g, w_in, forget_bias,
           conv_w, group_norm_g, w_out, norm3_g, ffn2_w_gate, ffn2_w_up, ffn2_w_down, final_g):
    B, S, D = x.shape
    assert S % FFN_TM == 0 and S % PROJ_TM == 0 and S % ATT_TQ == 0 and ATT_TQ == ATT_TK
    mod = _adaln(c, ada_w, ada_b).reshape(B, N_MOD, D)
    x1 = _ffn(x.reshape(B * S, D), mod, norm1_g, ffn1_w_gate.astype(bf16), ffn1_w_up.astype(bf16),
              ffn1_w_down.astype(bf16), mod_base=0, seq=S)
    q, k, v, qe, ke, cvn = _inproj(x1.reshape(B, S, D), mod, norm2_g, w_in, forget_bias, conv_w,
                                   group_norm_g[ATT_WIDTH:])
    att = _attention(q, k, v, qe, ke, group_norm_g[:ATT_WIDTH])
    out = _ffn(x1, mod, norm3_g, ffn2_w_gate.astype(bf16), ffn2_w_up.astype(bf16), ffn2_w_down.astype(bf16),
               mod_base=6, seq=S,
               mixer=(att.reshape(B * S, ATT_WIDTH), cvn.reshape(B * S, ATT_WIDTH), w_out.astype(bf16)),
               final_g=final_g)
    return out.reshape(B, S, D)
```

```python
import functools

import numpy as np
import jax
import jax.numpy as jnp
from jax import lax
from jax.experimental import pallas as pl
from jax.experimental.pallas import tpu as pltpu

HEAD_DIM = 64
N_ATT_HEADS = 8
ATT_WIDTH = N_ATT_HEADS * HEAD_DIM
CONV_K = 3
N_MOD = 9
EPS = 1e-6

LANES = 128
HEADS_PER_BLOCK = LANES // HEAD_DIM
EXTRA_PER_HEAD = 8
VMEM_LIMIT = 56 * 1024 * 1024

FFN_TM = 512
FFN_TF = 256
PROJ_TM = 512
CUMSUM_ROWS = 256
ATT_TQ = 512
ATT_TK = 512
ATT_RC = 64
ATT_STREAMS = 2
NEG = -1e30
LOG2E = float(np.log2(np.e))

bf16 = jnp.bfloat16
f32 = jnp.float32


def _dot(a, b):
    return jnp.dot(a, b, preferred_element_type=f32)


def _split3(x):
    hi = x.astype(bf16)
    r = x - hi.astype(f32)
    mid = r.astype(bf16)
    lo = (r - mid.astype(f32)).astype(bf16)
    return hi, mid, lo


def _split2(x):
    hi = x.astype(bf16)
    return hi, (x - hi.astype(f32)).astype(bf16)


def _norm_modulate(x, g, shift, scale):
    ms = jnp.mean(x * x, axis=-1, keepdims=True)
    y = x * lax.rsqrt(ms + EPS) * g
    return y * (1.0 + scale) + shift


def _resident(shape):
    return pl.BlockSpec(shape, lambda *_: (0,) * len(shape), pipeline_mode=pl.Buffered(1))


def _adaln_kernel(c_ref, w_ref, b_ref, o_ref):
    c = c_ref[...]
    sc = (c * (1.0 / (1.0 + jnp.exp(-c)))).astype(bf16)
    o_ref[...] = _dot(sc, w_ref[...].astype(bf16)) + b_ref[...]


def _adaln(c, ada_w, ada_b):
    B, D = c.shape
    n = ada_w.shape[1]
    return pl.pallas_call(
        _adaln_kernel,
        grid=(n // D,),
        in_specs=[pl.BlockSpec((B, D), lambda j: (0, 0)),
                  pl.BlockSpec((D, D), lambda j: (0, j)),
                  pl.BlockSpec((1, D), lambda j: (0, j))],
        out_specs=pl.BlockSpec((B, D), lambda j: (0, j)),
        out_shape=jax.ShapeDtypeStruct((B, n), f32),
        compiler_params=pltpu.CompilerParams(dimension_semantics=("arbitrary",),
                                             vmem_limit_bytes=VMEM_LIMIT),
        name="adaln",
    )(c, ada_w, ada_b.reshape(1, n))


def _ffn_kernel(*refs, mod_base, with_mixer, with_final):
    it = iter(refs)
    x_ref, mod_ref, ng_ref, wg_ref, wu_ref, wd_ref = (next(it) for _ in range(6))
    if with_mixer:
        att_ref, cvn_ref, wo_ref = (next(it) for _ in range(3))
    if with_final:
        fg_ref = next(it)
    o_ref = next(it)
    a_ref = next(it)

    x = x_ref[...]
    if with_mixer:
        mix = _dot(att_ref[...], wo_ref[:ATT_WIDTH, :]) + _dot(cvn_ref[...], wo_ref[ATT_WIDTH:, :])
        x = x + mod_ref[0, mod_base - 1:mod_base, :] * mix
    shift = mod_ref[0, mod_base:mod_base + 1, :]
    scale = mod_ref[0, mod_base + 1:mod_base + 2, :]
    gate = mod_ref[0, mod_base + 2:mod_base + 3, :]
    h = _norm_modulate(x, ng_ref[...], shift, scale).astype(bf16)
    d_ff = wg_ref.shape[1]
    for c0 in range(0, d_ff, FFN_TF):
        g = _dot(h, wg_ref[:, c0:c0 + FFN_TF])
        u = _dot(h, wu_ref[:, c0:c0 + FFN_TF])
        a_ref[:, c0:c0 + FFN_TF] = (g * (1.0 / (1.0 + jnp.exp(-g))) * u).astype(bf16)
    y = x + (0.5 * gate) * _dot(a_ref[...], wd_ref[...])
    if with_final:
        ms = jnp.mean(y * y, axis=-1, keepdims=True)
        y = y * lax.rsqrt(ms + EPS) * fg_ref[...]
    o_ref[...] = y


def _ffn(x2d, mod, norm_g, wg, wu, wd, *, mod_base, seq, mixer=None, final_g=None):
    T, D = x2d.shape
    d_ff = wg.shape[1]
    tm = FFN_TM
    blocks_per_seq = seq // tm
    row = lambda i: (i, 0)
    args = [x2d, mod, norm_g.reshape(1, D), wg, wu, wd]
    in_specs = [pl.BlockSpec((tm, D), row),
                pl.BlockSpec((1, N_MOD, D), lambda i: (i // blocks_per_seq, 0, 0)),
                _resident((1, D)), _resident((D, d_ff)), _resident((D, d_ff)), _resident((d_ff, D))]
    if mixer is not None:
        att, cvn, wo = mixer
        args += [att, cvn, wo]
        in_specs += [pl.BlockSpec((tm, att.shape[1]), row), pl.BlockSpec((tm, cvn.shape[1]), row),
                     _resident(wo.shape)]
    if final_g is not None:
        args.append(final_g.reshape(1, D))
        in_specs.append(_resident((1, D)))
    kern = functools.partial(_ffn_kernel, mod_base=mod_base, with_mixer=mixer is not None,
                             with_final=final_g is not None)
    return pl.pallas_call(
        kern,
        grid=(T // tm,),
        in_specs=in_specs,
        out_specs=pl.BlockSpec((tm, D), row),
        out_shape=jax.ShapeDtypeStruct((T, D), f32),
        scratch_shapes=[pltpu.VMEM((tm, d_ff), bf16)],
        compiler_params=pltpu.CompilerParams(dimension_semantics=("arbitrary",),
                                             vmem_limit_bytes=VMEM_LIMIT),
        name="ffn_mix" if mixer is not None else "ffn",
    )(*args)


def _inproj_kernel(x_ref, mod_ref, ng_ref, wqkv_ref, wfbcx_ref, fb_ref, cw_ref, gg_ref,
                   tri_ref, pqk_ref, oqk_ref, gsum_ref, gexp_ref,
                   q_ref, k_ref, v_ref, qe_ref, ke_ref, cvn_ref,
                   fcarry_ref, u_ref):
    j = pl.program_id(1)
    tm = x_ref.shape[1]
    W = ATT_WIDTH

    @pl.when(j == 0)
    def _():
        fcarry_ref[...] = jnp.zeros_like(fcarry_ref)
        u_ref[tm:tm + 8, :] = jnp.zeros((8, W), f32)

    x = x_ref[0]
    h = _norm_modulate(x, ng_ref[...], mod_ref[0, 3:4, :], mod_ref[0, 4:5, :]).astype(bf16)

    fbcx = _dot(h, wfbcx_ref[...])
    q_ref[0] = _dot(h, wqkv_ref[:, :W]).astype(bf16)

    fl = fbcx[:, :LANES] + fb_ref[...]
    lf = jnp.minimum(fl, 0.0) - jnp.log1p(jnp.exp(-jnp.abs(fl)))
    lane = lax.broadcasted_iota(jnp.int32, lf.shape, 1)
    lf = jnp.where(lane < N_ATT_HEADS, lf * LOG2E, 0.0)
    l3 = jnp.concatenate(_split3(lf), axis=1)
    tri = tri_ref[...]
    carry = fcarry_ref[...]
    f_parts = []
    for r0 in range(0, tm, CUMSUM_ROWS):
        part = _dot(tri, l3[r0:r0 + CUMSUM_ROWS, :])
        f_sub = part[:, :LANES] + part[:, LANES:2 * LANES] + part[:, 2 * LANES:] + carry
        carry = f_sub[CUMSUM_ROWS - 1:CUMSUM_ROWS, :]
        f_parts.append(f_sub)
    fcarry_ref[...] = carry
    F = jnp.concatenate(f_parts, axis=0)

    k_ref[0] = _dot(h, wqkv_ref[:, W:2 * W]).astype(bf16)

    f3 = jnp.concatenate(_split3(F), axis=1)
    qke = _dot(f3, pqk_ref[...]) + oqk_ref[...]
    qe_ref[0] = qke[:, :LANES].astype(bf16)
    ke_ref[0] = qke[:, LANES:].astype(bf16)

    gb = fbcx[:, LANES:LANES + W]
    u = fbcx[:, LANES + W:LANES + 2 * W] * fbcx[:, LANES + 2 * W:]
    u_ref[0:8, :] = u_ref[tm:tm + 8, :]
    u_ref[8:tm + 8, :] = u
    y = cw_ref[0:1, :] * u_ref[6:6 + tm, :] + cw_ref[1:2, :] * u_ref[7:7 + tm, :] + cw_ref[2:3, :] * u
    cv = gb * y
    s_hi, s_lo = _split2(cv * cv)
    gsum = gsum_ref[...]
    ss = _dot(s_hi, gsum) + _dot(s_lo, gsum)

    v_ref[0] = _dot(h, wqkv_ref[:, 2 * W:]).astype(bf16)

    inv = lax.rsqrt(ss * (1.0 / HEAD_DIM) + EPS)
    i_hi, i_lo = _split2(inv)
    gexp = gexp_ref[...]
    invx = _dot(i_hi, gexp) + _dot(i_lo, gexp)
    cvn_ref[0] = (cv * invx * gg_ref[...]).astype(bf16)


def _bias_lane_tables():
    pq = np.zeros((3 * LANES, LANES), np.float32)
    pk = np.zeros((3 * LANES, LANES), np.float32)
    oq = np.zeros((1, LANES), np.float32)
    ok = np.zeros((1, LANES), np.float32)
    for h in range(N_ATT_HEADS):
        for part in range(3):
            pq[part * LANES + h, EXTRA_PER_HEAD * h + part] = 1.0
            pk[part * LANES + h, EXTRA_PER_HEAD * h + 3 + part] = -1.0
            oq[0, EXTRA_PER_HEAD * h + 3 + part] = 1.0
            ok[0, EXTRA_PER_HEAD * h + part] = 1.0
    return np.concatenate([pq, pk], axis=1), np.concatenate([oq, ok], axis=1)


def _inproj(x1, mod, norm_g, w_in, forget_bias, conv_w, gn_conv):
    B, S, D = x1.shape
    W = ATT_WIDTH
    tm = PROJ_TM
    scale = LOG2E / np.sqrt(HEAD_DIM)
    wqkv = jnp.concatenate([w_in[:, :W] * scale, w_in[:, W:3 * W]], axis=1).astype(bf16)
    wfbcx = jnp.concatenate([w_in[:, 3 * W:3 * W + LANES], w_in[:, 3 * W + N_ATT_HEADS:]], axis=1).astype(bf16)
    fb = jnp.zeros((1, LANES), f32).at[0, :N_ATT_HEADS].set(forget_bias)
    pqk, oqk = _bias_lane_tables()
    tri = np.tril(np.ones((CUMSUM_ROWS, CUMSUM_ROWS), np.float32))
    group = np.arange(W) // HEAD_DIM
    gsum = (group[:, None] == np.arange(LANES)[None, :]).astype(np.float32)
    blk = lambda b, j: (b, j, 0)
    out_w = lambda w, dt: jax.ShapeDtypeStruct((B, S, w), dt)
    return pl.pallas_call(
        _inproj_kernel,
        grid=(B, S // tm),
        in_specs=[pl.BlockSpec((1, tm, D), blk),
                  pl.BlockSpec((1, N_MOD, D), lambda b, j: (b, 0, 0)),
                  _resident((1, D)), _resident((D, 3 * W)), _resident((D, 3 * W + LANES)),
                  _resident((1, LANES)), _resident((CONV_K, W)), _resident((1, W)),
                  _resident((CUMSUM_ROWS, CUMSUM_ROWS)), _resident((3 * LANES, 2 * LANES)),
                  _resident((1, 2 * LANES)), _resident((W, LANES)), _resident((LANES, W))],
        out_specs=[pl.BlockSpec((1, tm, W), blk)] * 3 + [pl.BlockSpec((1, tm, LANES), blk)] * 2
                  + [pl.BlockSpec((1, tm, W), blk)],
        out_shape=[out_w(W, bf16)] * 3 + [out_w(LANES, bf16)] * 2 + [out_w(W, bf16)],
        scratch_shapes=[pltpu.VMEM((1, LANES), f32), pltpu.VMEM((tm + 8, W), f32)],
        compiler_params=pltpu.CompilerParams(dimension_semantics=("arbitrary", "arbitrary"),
                                             vmem_limit_bytes=VMEM_LIMIT),
        name="inproj",
    )(x1, mod, norm_g.reshape(1, D), wqkv, wfbcx, fb, conv_w, gn_conv.reshape(1, W),
      jnp.asarray(tri, bf16), jnp.asarray(pqk, bf16), jnp.asarray(oqk),
      jnp.asarray(gsum, bf16), jnp.asarray(gsum.T.copy(), bf16))


def _attention_kernel(q_ref, k_ref, v_ref, qe_ref, ke_ref, gg_ref, o_ref,
                      qa_scr, s_scr, p_scr, m_scr, l_scr, a_scr, acc_scr):
    S = q_ref.shape[1]
    tq, tk, rc = ATT_TQ, ATT_TK, ATT_RC
    R = HEADS_PER_BLOCK * tq
    streams = range(ATT_STREAMS)
    lane = lax.broadcasted_iota(jnp.int32, (1, LANES), 1)
    first_head = lane < HEAD_DIM
    contract_last = (((1,), (1,)), ((), ()))

    def softmax_block(t, diagonal):
        for r0 in range(0, R, rc):
            rq = r0 % tq
            rows = slice(r0, r0 + rc)
            visible, hidden = [], []
            for c0 in range(0, tk, LANES):
                cols = slice(c0, c0 + LANES)
                if diagonal and c0 > rq + rc - 1:
                    hidden.append(cols)
                    continue
                s = s_scr[t, rows, cols]
                if diagonal and c0 + LANES - 1 > rq:
                    qpos = rq + lax.broadcasted_iota(jnp.int32, (rc, LANES), 0)
                    kpos = c0 + lax.broadcasted_iota(jnp.int32, (rc, LANES), 1)
                    s = jnp.where(kpos <= qpos, s, NEG)
                visible.append((cols, s))
            mx = visible[0][1]
            for _, s in visible[1:]:
                mx = jnp.maximum(mx, s)
            m_prev = m_scr[t, rows, :]
            m_new = jnp.maximum(m_prev, jnp.max(mx, axis=-1, keepdims=True))
            alpha = jnp.exp2(m_prev - m_new)
            tot = None
            for cols, s in visible:
                pr = jnp.exp2(s - m_new)
                tot = pr if tot is None else tot + pr
                p_scr[t, rows, cols] = pr.astype(bf16)
            for cols in hidden:
                p_scr[t, rows, cols] = jnp.zeros((rc, LANES), bf16)
            l_scr[t, rows, :] = alpha * l_scr[t, rows, :] + tot
            m_scr[t, rows, :] = m_new
            a_scr[t, rows, :] = alpha

    def kv_step(kb, diagonal):
        k0 = pl.multiple_of(kb * tk, tk)
        ke = ke_ref[0, pl.ds(k0, tk), :]
        for t in streams:
            ka = jnp.concatenate([k_ref[0, pl.ds(k0, tk), t * LANES:(t + 1) * LANES], ke], axis=1)
            s_scr[t] = lax.dot_general(qa_scr[t], ka, contract_last, preferred_element_type=f32)
        for t in streams:
            softmax_block(t, diagonal)
            pv = _dot(p_scr[t], v_ref[0, pl.ds(k0, tk), t * LANES:(t + 1) * LANES])
            acc_scr[t] = a_scr[t] * acc_scr[t] + pv

    def q_block(i, _):
        q0 = pl.multiple_of(i * tq, tq)
        qe = qe_ref[0, pl.ds(q0, tq), :]
        for t in streams:
            q = q_ref[0, pl.ds(q0, tq), t * LANES:(t + 1) * LANES]
            zq = jnp.zeros_like(q)
            pair = pl.program_id(1) * ATT_STREAMS + t
            for hh in range(HEADS_PER_BLOCK):
                rows = slice(hh * tq, (hh + 1) * tq)
                qa_scr[t, rows, :LANES] = jnp.where((lane >= HEAD_DIM * hh) & (lane < HEAD_DIM * (hh + 1)), q, zq)
                lo = EXTRA_PER_HEAD * (HEADS_PER_BLOCK * pair + hh)
                qa_scr[t, rows, LANES:] = jnp.where((lane >= lo) & (lane < lo + EXTRA_PER_HEAD), qe, zq)
        m_scr[...] = jnp.full(m_scr.shape, NEG, f32)
        l_scr[...] = jnp.zeros(l_scr.shape, f32)
        acc_scr[...] = jnp.zeros(acc_scr.shape, f32)

        def full_step(kb, carry):
            kv_step(kb, False)
            return carry

        lax.fori_loop(0, i, full_step, 0)
        kv_step(i, True)

        for t in streams:
            o2 = acc_scr[t] / jnp.sum(l_scr[t], axis=-1, keepdims=True)
            o = jnp.where(first_head, o2[:tq], o2[tq:])
            sq = o * o
            ss0 = jnp.sum(jnp.where(first_head, sq, 0.0), axis=-1, keepdims=True)
            ss1 = jnp.sum(jnp.where(first_head, 0.0, sq), axis=-1, keepdims=True)
            ms = jnp.where(first_head, ss0, ss1) * (1.0 / HEAD_DIM)
            y = o * lax.rsqrt(ms + EPS) * gg_ref[:, t * LANES:(t + 1) * LANES]
            o_ref[0, pl.ds(q0, tq), t * LANES:(t + 1) * LANES] = y.astype(bf16)
        return 0

    lax.fori_loop(0, S // tq, q_block, 0)


def _attention(q, k, v, qe, ke, gn_att):
    B, S, W = q.shape
    R = HEADS_PER_BLOCK * ATT_TQ
    wb = ATT_STREAMS * LANES
    group = lambda b, p: (b, 0, p)
    whole = lambda b, p: (b, 0, 0)
    per_stream = lambda cols, dt: pltpu.VMEM((ATT_STREAMS, R, cols), dt)
    return pl.pallas_call(
        _attention_kernel,
        grid=(B, W // wb),
        in_specs=[pl.BlockSpec((1, S, wb), group)] * 3 + [pl.BlockSpec((1, S, LANES), whole)] * 2
                 + [pl.BlockSpec((1, wb), lambda b, p: (0, p))],
        out_specs=pl.BlockSpec((1, S, wb), group),
        out_shape=jax.ShapeDtypeStruct((B, S, W), bf16),
        scratch_shapes=[per_stream(2 * LANES, bf16), per_stream(ATT_TK, f32), per_stream(ATT_TK, bf16)]
                       + [per_stream(LANES, f32)] * 4,
        compiler_params=pltpu.CompilerParams(dimension_semantics=("arbitrary", "arbitrary"),
                                             vmem_limit_bytes=VMEM_LIMIT),
        name="attention",
    )(q, k, v, qe, ke, gn_att.reshape(1, W))


def kernel(x, c, ada_w, ada_b, norm1_g, ffn1_w_gate, ffn1_w_up, ffn1_w_down, norm2_g, w_in, forget_bias,
           conv_w, group_norm_g, w_out, norm3_g, ffn2_w_gate, ffn2_w_up, ffn2_w_down, final_g):
    B, S, D = x.shape
    assert S % FFN_TM == 0 and S % PROJ_TM == 0 and S % ATT_TQ == 0 and ATT_TQ == ATT_TK
    mod = _adaln(c, ada_w, ada_b).reshape(B, N_MOD, D)
    x1 = _ffn(x.reshape(B * S, D), mod, norm1_g, ffn1_w_gate.astype(bf16), ffn1_w_up.astype(bf16),
              ffn1_w_down.astype(bf16), mod_base=0, seq=S)
    q, k, v, qe, ke, cvn = _inproj(x1.reshape(B, S, D), mod, norm2_g, w_in, forget_bias, conv_w,
                                   group_norm_g[ATT_WIDTH:])
    att = _attention(q, k, v, qe, ke, group_norm_g[:ATT_WIDTH])
    out = _ffn(x1, mod, norm3_g, ffn2_w_gate.astype(bf16), ffn2_w_up.astype(bf16), ffn2_w_down.astype(bf16),
               mod_base=6, seq=S,
               mixer=(att.reshape(B * S, ATT_WIDTH), cvn.reshape(B * S, ATT_WIDTH), w_out.astype(bf16)),
               final_g=final_g)
    return out.reshape(B, S, D)
```
